```python
import math
import jax, jax.numpy as jnp
from jax import lax
import numpy as np

D_MODEL = 1024
BATCH = 32
SEQ = 2048
DEPTH = 1

CHUNK = 64
Q_BLOCK = 128
N_HEADS = 4
HEAD_DIM = 64
V_HEAD_DIM = 2 * HEAD_DIM
ATTN_WIDTH = N_HEADS * V_HEAD_DIM
CONV_CH = D_MODEL // 2
CONV_K = 31
FFN_DIM = 2816
FFN_CONV_K = 3
ROPE_THETA = 10000.0
RMS_EPS = 1e-6
SUB_EPS = 1e-5
LN_EPS = 1e-5

Q_COLS = N_HEADS * 2 * HEAD_DIM
K_COLS = N_HEADS * 2 * HEAD_DIM
V_COLS = ATTN_WIDTH
GLU_COLS = 2 * CONV_CH
GATE_COLS = 2 * D_MODEL
IN_COLS = Q_COLS + K_COLS + V_COLS + GLU_COLS + GATE_COLS
SPLITS = [Q_COLS, Q_COLS + K_COLS, Q_COLS + K_COLS + V_COLS, Q_COLS + K_COLS + V_COLS + GLU_COLS]

kernel_name = "hybrid_diffattn_conformer_convffn"


def rms_norm(x, g, eps=RMS_EPS):
    xf = x.astype(jnp.float32)
    y = xf * lax.rsqrt(jnp.mean(xf * xf, axis=-1, keepdims=True) + eps)
    return (y * g.astype(jnp.float32)).astype(x.dtype)


def layer_norm(x, g, b, eps=LN_EPS):
    xf = x.astype(jnp.float32)
    mu = jnp.mean(xf, axis=-1, keepdims=True)
    var = jnp.mean(jnp.square(xf - mu), axis=-1, keepdims=True)
    y = (xf - mu) * lax.rsqrt(var + eps)
    return (y * g.astype(jnp.float32) + b.astype(jnp.float32)).astype(x.dtype)


def rope(x, pos):
    dh = x.shape[-1]
    inv_freq = ROPE_THETA ** (-jnp.arange(0, dh, 2, dtype=jnp.float32) / dh)
    ang = pos.astype(jnp.float32)[:, None] * inv_freq[None, :]
    ang = jnp.concatenate([ang, ang], axis=-1)
    cos = jnp.cos(ang)[None, :, None, :]
    sin = jnp.sin(ang)[None, :, None, :]
    xf = x.astype(jnp.float32)
    x1, x2 = jnp.split(xf, 2, axis=-1)
    rot = jnp.concatenate([-x2, x1], axis=-1)
    return (xf * cos + rot * sin).astype(x.dtype)


def causal_depthwise_conv(x, w, b):
    k_width, ch = w.shape
    y = lax.conv_general_dilated(
        x, w.astype(x.dtype)[:, None, :], window_strides=(1,),
        padding=[(k_width - 1, 0)], dimension_numbers=("NWC", "WIO", "NWC"),
        feature_group_count=ch)
    return y + b.astype(x.dtype)


def diff_attention(q, k, v, lam, g_sub, lam_init):
    bsz, seq = q.shape[0], q.shape[1]
    scale = HEAD_DIM ** -0.5
    chunk_id = jnp.arange(seq) // CHUNK
    outs = []
    for s0 in range(0, seq, Q_BLOCK):
        kend = s0 + Q_BLOCK
        qb = q[:, s0:kend].astype(jnp.float32)
        kb = k[:, :kend].astype(jnp.float32)
        s = jnp.einsum("bqhmd,bkhmd->bhmqk", qb, kb) * scale
        mask = chunk_id[None, :kend] <= chunk_id[s0:kend, None]
        s = jnp.where(mask, s, -jnp.inf)
        p = jax.nn.softmax(s, axis=-1)
        a = p[:, :, 0] - lam * p[:, :, 1]
        outs.append(jnp.einsum("bhqk,bkhe->bqhe", a, v[:, :kend].astype(jnp.float32)))
    o = jnp.concatenate(outs, axis=1)
    o = o * lax.rsqrt(jnp.mean(o * o, axis=-1, keepdims=True) + SUB_EPS)
    o = o * g_sub.astype(jnp.float32) * (1.0 - lam_init)
    return o.reshape(bsz, seq, ATTN_WIDTH).astype(v.dtype)


def setup_inputs(seed: int = 0) -> dict:
    key = jax.random.key(seed)
    ks = jax.random.split(key, 32)
    f32 = jnp.float32
    L = DEPTH

    def nrm(k, shape, scale):
        return jax.random.normal(k, shape, f32) * scale

    def gain(k, shape):
        return 1.0 + 0.02 * jax.random.normal(k, shape, f32)

    return {
        "x": jax.random.normal(ks[0], (BATCH, SEQ, D_MODEL), f32),
        "g_mix_pre": gain(ks[1], (L, D_MODEL)),
        "w_in": nrm(ks[2], (L, D_MODEL, IN_COLS), D_MODEL ** -0.5),
        "lambda_q1": nrm(ks[3], (L, HEAD_DIM), 0.1),
        "lambda_k1": nrm(ks[4], (L, HEAD_DIM), 0.1),
        "lambda_q2": nrm(ks[5], (L, HEAD_DIM), 0.1),
        "lambda_k2": nrm(ks[6], (L, HEAD_DIM), 0.1),
        "g_sub": gain(ks[7], (L, V_HEAD_DIM)),
        "w_attn_o": nrm(ks[8], (L, ATTN_WIDTH, D_MODEL), ATTN_WIDTH ** -0.5),
        "w_dw": nrm(ks[9], (L, CONV_K, CONV_CH), CONV_K ** -0.5),
        "b_dw": nrm(ks[10], (L, CONV_CH), 0.02),
        "ln_g": gain(ks[11], (L, CONV_CH)),
        "ln_b": nrm(ks[12], (L, CONV_CH), 0.02),
        "w_conv_o": nrm(ks[13], (L, CONV_CH, D_MODEL), CONV_CH ** -0.5),
        "b_gate": nrm(ks[14], (L, GATE_COLS), 0.02),
        "w_out": nrm(ks[15], (L, D_MODEL, D_MODEL), D_MODEL ** -0.5),
        "g_mix_post": gain(ks[16], (L, D_MODEL)),
        "g_ffn_pre": gain(ks[17], (L, D_MODEL)),
        "w_up": nrm(ks[18], (L, D_MODEL, 2 * FFN_DIM), D_MODEL ** -0.5),
        "w_fconv": nrm(ks[19], (L, FFN_CONV_K, FFN_DIM), FFN_CONV_K ** -0.5),
        "b_fconv": nrm(ks[20], (L, FFN_DIM), 0.02),
        "w_down": nrm(ks[21], (L, FFN_DIM, D_MODEL), FFN_DIM ** -0.5),
        "g_ffn_post": gain(ks[22], (L, D_MODEL)),
    }


def reference(x, g_mix_pre, w_in, lambda_q1, lambda_k1, lambda_q2, lambda_k2, g_sub,
              w_attn_o, w_dw, b_dw, ln_g, ln_b, w_conv_o, b_gate, w_out, g_mix_post,
              g_ffn_pre, w_up, w_fconv, b_fconv, w_down, g_ffn_post):
    bsz, seq, _ = x.shape
    pos = jnp.arange(seq, dtype=jnp.int32)
    for l in range(DEPTH):
        lam_init = 0.8 - 0.6 * math.exp(-0.3 * l)
        h = rms_norm(x, g_mix_pre[l])
        z = h @ w_in[l]
        q, k, v, glu, gate = jnp.split(z, SPLITS, axis=-1)
        q = rope(q.reshape(bsz, seq, 2 * N_HEADS, HEAD_DIM), pos).reshape(bsz, seq, N_HEADS, 2, HEAD_DIM)
        k = rope(k.reshape(bsz, seq, 2 * N_HEADS, HEAD_DIM), pos).reshape(bsz, seq, N_HEADS, 2, HEAD_DIM)
        v = v.reshape(bsz, seq, N_HEADS, V_HEAD_DIM)
        lam = (jnp.exp(jnp.sum(lambda_q1[l].astype(jnp.float32) * lambda_k1[l].astype(jnp.float32)))
               - jnp.exp(jnp.sum(lambda_q2[l].astype(jnp.float32) * lambda_k2[l].astype(jnp.float32)))
               + lam_init)
        y_a = diff_attention(q, k, v, lam, g_sub[l], lam_init) @ w_attn_o[l]
        c = glu[..., :CONV_CH] * jax.nn.sigmoid(glu[..., CONV_CH:])
        c = causal_depthwise_conv(c, w_dw[l], b_dw[l])
        c = jax.nn.silu(layer_norm(c, ln_g[l], ln_b[l]))
        y_c = c @ w_conv_o[l]
        g = jax.nn.sigmoid(gate + b_gate[l])
        m = g[..., :D_MODEL] * y_a + g[..., D_MODEL:] * y_c
        x = x + rms_norm(m @ w_out[l], g_mix_post[l])
        h = rms_norm(x, g_ffn_pre[l])
        u = h @ w_up[l]
        a, b = u[..., :FFN_DIM], u[..., FFN_DIM:]
        a = causal_depthwise_conv(a, w_fconv[l], b_fconv[l])
        f = (jax.nn.gelu(a, approximate=True) * b) @ w_down[l]
        x = x + rms_norm(f, g_ffn_post[l])
    return x
```

```python
import functools
import math

import jax
import jax.numpy as jnp
from jax import lax
from jax.experimental import pallas as pl
from jax.experimental.pallas import tpu as pltpu

N_HEADS = 4
HEAD_DIM = 64
V_HEAD_DIM = 2 * HEAD_DIM
CHUNK = 64
CONV_K = 31
FFN_CONV_K = 3
ROPE_THETA = 10000.0
RMS_EPS = 1e-6
SUB_EPS = 1e-5
LN_EPS = 1e-5
LAM_INIT = 0.8 - 0.6 * math.exp(-0.3 * 0)

LANES = 128
SUBLANES = 8
VMEM_LIMIT = 56 * 1024 * 1024

TM_PROJ = 512
TM_MIX = 512
TM_FFN = 256
TQ = 256
HALO = 32
NEG_BIG = -1e30

BF16 = jnp.bfloat16
F32 = jnp.float32


def _rms_norm(x, g, eps):
    ms = jnp.mean(x * x, axis=-1, keepdims=True)
    return (x * lax.rsqrt(ms + eps)) * g


def _const_spec(shape):
    nd = len(shape)
    return pl.BlockSpec(shape, lambda *_: (0,) * nd, pipeline_mode=pl.Buffered(1))


def _inproj_kernel(x_ref, g_ref, w_ref, cos_ref, sin_lo_ref, sin_hi_ref,
                   q_ref, k_ref, v_ref, c_ref):
    width = q_ref.shape[-1]
    h = _rms_norm(x_ref[...], g_ref[...], RMS_EPS).astype(BF16)
    z = jnp.dot(h, w_ref[...], preferred_element_type=F32)
    cos = cos_ref[...]
    sin_lo = sin_lo_ref[...]
    sin_hi = sin_hi_ref[...]

    def rope_store(dst_ref, base):
        for s in range(width // LANES):
            zs = z[:, base + s * LANES: base + (s + 1) * LANES]
            up = pltpu.roll(zs, LANES - HEAD_DIM // 2, 1)
            down = pltpu.roll(zs, HEAD_DIM // 2, 1)
            r = zs * cos + up * sin_lo + down * sin_hi
            dst_ref[:, s * LANES:(s + 1) * LANES] = r.astype(dst_ref.dtype)

    rope_store(q_ref, 0)
    rope_store(k_ref, width)
    v_ref[...] = z[:, 2 * width:3 * width].astype(v_ref.dtype)
    a = z[:, 3 * width:4 * width]
    b = z[:, 4 * width:5 * width]
    c_ref[...] = (a * jax.nn.sigmoid(b)).astype(c_ref.dtype)


def _inproj(x2d, g, w, cos, sin_lo, sin_hi, seq):
    tokens, d_model = x2d.shape
    width = N_HEADS * V_HEAD_DIM
    tm = TM_PROJ
    tiles_per_seq = seq // tm
    row = lambda i: (i, 0)
    pos = lambda i: (i % tiles_per_seq, 0)
    out = jax.ShapeDtypeStruct((tokens, width), BF16)
    return pl.pallas_call(
        _inproj_kernel,
        grid=(tokens // tm,),
        in_specs=[
            pl.BlockSpec((tm, d_model), row),
            _const_spec((1, d_model)),
            _const_spec(w.shape),
            pl.BlockSpec((tm, LANES), pos),
            pl.BlockSpec((tm, LANES), pos),
            pl.BlockSpec((tm, LANES), pos),
        ],
        out_specs=[pl.BlockSpec((tm, width), row)] * 4,
        out_shape=[out] * 4,
        compiler_params=pltpu.CompilerParams(
            dimension_semantics=("parallel",), vmem_limit_bytes=VMEM_LIMIT),
        name="inproj",
    )(x2d, g, w, cos, sin_lo, sin_hi)


def _attn_kernel(lq1_ref, lk1_ref, lq2_ref, lk2_ref, gsub_ref, q_ref, k_ref, v_ref,
                 o_ref, acc_ref):
    seq = q_ref.shape[1]
    tq = TQ
    nq = seq // tq
    lam = (jnp.exp(jnp.sum(lq1_ref[...] * lk1_ref[...], axis=-1, keepdims=True))
           - jnp.exp(jnp.sum(lq2_ref[...] * lk2_ref[...], axis=-1, keepdims=True))
           + LAM_INIT)
    gsub = gsub_ref[...] * (1.0 - LAM_INIT)

    lane = lax.broadcasted_iota(jnp.int32, (tq, V_HEAD_DIM), 1)
    row_chunk = (lax.broadcasted_iota(jnp.int32, (2 * tq, tq), 0) & (tq - 1)) // CHUNK
    col_chunk = lax.broadcasted_iota(jnp.int32, (2 * tq, tq), 1) // CHUNK
    diag_mask = col_chunk <= row_chunk

    def block(qs, j, m, l, mask):
        kb = k_ref[0, pl.ds(pl.multiple_of(j * tq, tq), tq), :]
        vb = v_ref[0, pl.ds(pl.multiple_of(j * tq, tq), tq), :]
        s = lax.dot_general(qs, kb, (((1,), (1,)), ((), ())), preferred_element_type=F32)
        if mask is not None:
            s = jnp.where(mask, s, NEG_BIG)
        m_new = jnp.maximum(m, jnp.max(s, axis=-1, keepdims=True))
        alpha = jnp.exp(m - m_new)
        p = jnp.exp(s - m_new)
        l_new = alpha * l + jnp.sum(p, axis=-1, keepdims=True)
        acc_ref[...] = alpha * acc_ref[...] + jnp.dot(
            p.astype(BF16), vb, preferred_element_type=F32)
        return m_new, l_new

    def q_block(i, carry):
        q = q_ref[0, pl.ds(pl.multiple_of(i * tq, tq), tq), :]
        zero = jnp.zeros_like(q)
        qs = jnp.concatenate([jnp.where(lane < HEAD_DIM, q, zero),
                              jnp.where(lane >= HEAD_DIM, q, zero)], axis=0)
        acc_ref[...] = jnp.zeros_like(acc_ref)
        m0 = jnp.full((2 * tq, 1), NEG_BIG, F32)
        l0 = jnp.zeros((2 * tq, 1), F32)
        m, l = lax.fori_loop(0, i, lambda j, ml: block(qs, j, ml[0], ml[1], None), (m0, l0))
        m, l = block(qs, i, m, l, diag_mask)
        o = acc_ref[...] / l
        o = o[:tq] - lam * o[tq:]
        o = o * lax.rsqrt(jnp.mean(o * o, axis=-1, keepdims=True) + SUB_EPS)
        o_ref[0, pl.ds(pl.multiple_of(i * tq, tq), tq), :] = (o * gsub).astype(o_ref.dtype)
        return carry

    lax.fori_loop(0, nq, q_block, 0)


def _attention(q, k, v, lq1, lk1, lq2, lk2, g_sub):
    bsz, seq, width = q.shape
    head = lambda b, h: (b, 0, h)
    blk = pl.BlockSpec((1, seq, V_HEAD_DIM), head)
    vec = lambda n: pl.BlockSpec((1, n), lambda b, h: (0, 0))
    return pl.pallas_call(
        _attn_kernel,
        grid=(bsz, N_HEADS),
        in_specs=[vec(HEAD_DIM)] * 4 + [vec(V_HEAD_DIM), blk, blk, blk],
        out_specs=blk,
        out_shape=jax.ShapeDtypeStruct((bsz, seq, width), BF16),
        scratch_shapes=[pltpu.VMEM((2 * TQ, V_HEAD_DIM), F32)],
        compiler_params=pltpu.CompilerParams(
            dimension_semantics=("parallel", "parallel"), vmem_limit_bytes=VMEM_LIMIT),
        name="diff_attn",
    )(lq1, lk1, lq2, lk2, g_sub, q, k, v)


def _mix_kernel(tiles_per_seq, x_ref, attn_ref, c_ref, halo_ref, g_pre_ref, w_gate_ref,
                b_gate_ref, w_dw_ref, b_dw_ref, ln_g_ref, ln_b_ref, w_attn_o_ref,
                w_conv_o_ref, w_out_ref, g_post_ref, o_ref, cbuf_ref):
    tm, d_model = x_ref.shape
    first = (pl.program_id(0) % tiles_per_seq) == 0
    halo = halo_ref[...].astype(F32)
    cbuf_ref[0:HALO, :] = jnp.where(first, jnp.zeros_like(halo), halo)
    cbuf_ref[HALO:, :] = c_ref[...].astype(F32)

    off = HALO - (CONV_K - 1)
    w_dw = w_dw_ref[...]
    conv = cbuf_ref[pl.ds(off, tm), :] * w_dw[0:1, :] + b_dw_ref[...]
    for j in range(1, CONV_K):
        conv = conv + cbuf_ref[pl.ds(off + j, tm), :] * w_dw[j:j + 1, :]

    mu = jnp.mean(conv, axis=-1, keepdims=True)
    cen = conv - mu
    var = jnp.mean(cen * cen, axis=-1, keepdims=True)
    y = cen * lax.rsqrt(var + LN_EPS) * ln_g_ref[...] + ln_b_ref[...]
    y = (y * jax.nn.sigmoid(y)).astype(BF16)
    y_c = jnp.dot(y, w_conv_o_ref[...], preferred_element_type=F32)
    y_a = jnp.dot(attn_ref[...], w_attn_o_ref[...], preferred_element_type=F32)

    x = x_ref[...]
    h = _rms_norm(x, g_pre_ref[...], RMS_EPS).astype(BF16)
    gate = jnp.dot(h, w_gate_ref[...], preferred_element_type=F32) + b_gate_ref[...]
    gate = jax.nn.sigmoid(gate)
    m = (gate[:, :d_model] * y_a + gate[:, d_model:] * y_c).astype(BF16)
    o = jnp.dot(m, w_out_ref[...], preferred_element_type=F32)
    o_ref[...] = x + _rms_norm(o, g_post_ref[...], RMS_EPS)


def _mix(x2d, attn2d, c2d, seq, g_pre, w_gate, b_gate, w_dw, b_dw, ln_g, ln_b,
         w_attn_o, w_conv_o, w_out, g_post):
    tokens, d_model = x2d.shape
    width = attn2d.shape[1]
    tm = TM_MIX
    tiles_per_seq = seq // tm
    row = lambda i: (i, 0)
    halo = lambda i: (jnp.maximum(i * (tm // HALO) - 1, 0), 0)
    consts = [g_pre, w_gate, b_gate, w_dw, b_dw, ln_g, ln_b, w_attn_o, w_conv_o, w_out, g_post]
    return pl.pallas_call(
        functools.partial(_mix_kernel, tiles_per_seq),
        grid=(tokens // tm,),
        in_specs=[
            pl.BlockSpec((tm, d_model), row),
            pl.BlockSpec((tm, width), row),
            pl.BlockSpec((tm, width), row),
            pl.BlockSpec((HALO, width), halo),
        ] + [_const_spec(c.shape) for c in consts],
        out_specs=pl.BlockSpec((tm, d_model), row),
        out_shape=jax.ShapeDtypeStruct((tokens, d_model), F32),
        scratch_shapes=[pltpu.VMEM((tm + HALO, width), F32)],
        compiler_params=pltpu.CompilerParams(
            dimension_semantics=("parallel",), vmem_limit_bytes=VMEM_LIMIT),
        name="mix",
    )(x2d, attn2d, c2d, c2d, *consts)


def _ffn_kernel(tiles_per_seq, x_ref, g_pre_ref, w_a_ref, w_b_ref, w_fc_ref, b_fc_ref,
                w_down_ref, g_post_ref, o_ref, abuf_ref):
    tm = x_ref.shape[0]
    first = (pl.program_id(0) % tiles_per_seq) == 0

    @pl.when(first)
    def _():
        abuf_ref[0:SUBLANES, :] = jnp.zeros((SUBLANES, abuf_ref.shape[1]), F32)

    x = x_ref[...]
    h = _rms_norm(x, g_pre_ref[...], RMS_EPS).astype(BF16)
    abuf_ref[SUBLANES:, :] = jnp.dot(h, w_a_ref[...], preferred_element_type=F32)
    b = jnp.dot(h, w_b_ref[...], preferred_element_type=F32)
    w_fc = w_fc_ref[...]
    off = SUBLANES - (FFN_CONV_K - 1)
    a = b_fc_ref[...] + abuf_ref[pl.ds(off, tm), :] * w_fc[0:1, :]
    for j in range(1, FFN_CONV_K):
        a = a + abuf_ref[pl.ds(off + j, tm), :] * w_fc[j:j + 1, :]
    abuf_ref[0:SUBLANES, :] = abuf_ref[tm:tm + SUBLANES, :]
    p = (jax.nn.gelu(a, approximate=True) * b).astype(BF16)
    f = jnp.dot(p, w_down_ref[...], preferred_element_type=F32)
    o_ref[...] = x + _rms_norm(f, g_post_ref[...], RMS_EPS)


def _ffn(x2d, seq, g_pre, w_a, w_b, w_fc, b_fc, w_down, g_post):
    tokens, d_model = x2d.shape
    ffn = w_a.shape[1]
    tm = TM_FFN
    tiles_per_seq = seq // tm
    row = lambda i: (i, 0)
    consts = [g_pre, w_a, w_b, w_fc, b_fc, w_down, g_post]
    return pl.pallas_call(
        functools.partial(_ffn_kernel, tiles_per_seq),
        grid=(tokens // tm,),
        in_specs=[pl.BlockSpec((tm, d_model), row)] + [_const_spec(c.shape) for c in consts],
        out_specs=pl.BlockSpec((tm, d_model), row),
        out_shape=jax.ShapeDtypeStruct((tokens, d_model), F32),
        scratch_shapes=[pltpu.VMEM((tm + SUBLANES, ffn), F32)],
        compiler_params=pltpu.CompilerParams(
            dimension_semantics=("arbitrary",), vmem_limit_bytes=VMEM_LIMIT),
        name="conv_ffn",
    )(x2d, *consts)


def _rope_tables(seq):
    half = HEAD_DIM // 2
    inv_freq = ROPE_THETA ** (-jnp.arange(0, HEAD_DIM, 2, dtype=F32) / HEAD_DIM)
    ang = jnp.arange(seq, dtype=jnp.int32).astype(F32)[:, None] * inv_freq[None, :]
    ang = jnp.concatenate([ang, ang], axis=-1)
    cos, sin = jnp.cos(ang), jnp.sin(ang)
    lo = (jnp.arange(HEAD_DIM) < half)[None, :]
    sin_lo = jnp.where(lo, -sin, 0.0)
    sin_hi = jnp.where(lo, 0.0, sin)
    rep = LANES // HEAD_DIM
    return tuple(jnp.tile(t, (1, rep)) for t in (cos, sin_lo, sin_hi))


def kernel(x, g_mix_pre, w_in, lambda_q1, lambda_k1, lambda_q2, lambda_k2, g_sub,
           w_attn_o, w_dw, b_dw, ln_g, ln_b, w_conv_o, b_gate, w_out, g_mix_post,
           g_ffn_pre, w_up, w_fconv, b_fconv, w_down, g_ffn_post):
    bsz, seq, d_model = x.shape
    depth = w_in.shape[0]
    assert depth == 1, "LAM_INIT is specialised to a single layer"
    assert seq % TM_PROJ == 0 and seq % TM_MIX == 0 and seq % TM_FFN == 0 and seq % TQ == 0
    width = N_HEADS * V_HEAD_DIM
    ffn = w_fconv.shape[-1]
    cos, sin_lo, sin_hi = _rope_tables(seq)
    x2d = x.reshape(bsz * seq, d_model)
    row2 = lambda a: a.reshape(1, -1)
    l = 0

    w_l = w_in[l]
    scale = HEAD_DIM ** -0.5
    w_proj = jnp.concatenate([w_l[:, :width] * scale, w_l[:, width:5 * width]], axis=1).astype(BF16)
    w_gate = w_l[:, 5 * width:].astype(BF16)

    q, k, v, c = _inproj(x2d, row2(g_mix_pre[l]), w_proj, cos, sin_lo, sin_hi, seq)
    attn = _attention(q.reshape(bsz, seq, width), k.reshape(bsz, seq, width),
                      v.reshape(bsz, seq, width), row2(lambda_q1[l]), row2(lambda_k1[l]),
                      row2(lambda_q2[l]), row2(lambda_k2[l]), row2(g_sub[l]))
    w_dw_pad = jnp.pad(w_dw[l], ((0, HALO - CONV_K), (0, 0)))
    x2d = _mix(x2d, attn.reshape(bsz * seq, width), c, seq, row2(g_mix_pre[l]), w_gate,
               row2(b_gate[l]), w_dw_pad, row2(b_dw[l]), row2(ln_g[l]), row2(ln_b[l]),
               w_attn_o[l].astype(BF16), w_conv_o[l].astype(BF16), w_out[l].astype(BF16),
               row2(g_mix_post[l]))
    w_fc_pad = jnp.pad(w_fconv[l], ((0, SUBLANES - FFN_CONV_K), (0, 0)))
    x2d = _ffn(x2d, seq, row2(g_ffn_pre[l]), w_up[l][:, :ffn].astype(BF16),
               w_up[l][:, ffn:].astype(BF16), w_fc_pad, row2(b_fconv[l]),
               w_down[l].astype(BF16), row2(g_ffn_post[l]))
    return x2d.reshape(bsz, seq, d_model)
```

```python
import functools
import math

import jax
import jax.numpy as jnp
from jax import lax
from jax.experimental import pallas as pl
from jax.experimental.pallas import tpu as pltpu

N_HEADS = 4
HEAD_DIM = 64
V_HEAD_DIM = 2 * HEAD_DIM
CHUNK = 64
CONV_K = 31
FFN_CONV_K = 3
ROPE_THETA = 10000.0
RMS_EPS = 1e-6
SUB_EPS = 1e-5
LN_EPS = 1e-5
LAM_INIT = 0.8 - 0.6 * math.exp(-0.3 * 0)

LANES = 128
SUBLANES = 8
VMEM_LIMIT = 56 * 1024 * 1024

TM_PROJ = 512
TM_MIX = 512
TM_FFN = 256
TQ = 256
DENOM_ROWS = 16
HALO = 32
NEG_BIG = -1e30

BF16 = jnp.bfloat16
F32 = jnp.float32


def _rms_norm(x, g, eps):
    ms = jnp.mean(x * x, axis=-1, keepdims=True)
    return (x * lax.rsqrt(ms + eps)) * g


def _const_spec(shape):
    nd = len(shape)
    return pl.BlockSpec(shape, lambda *_: (0,) * nd, pipeline_mode=pl.Buffered(1))


def _inproj_kernel(x_ref, g_ref, w_ref, rope_ref, qt_ref, k_ref, vt_ref, c_ref):
    tm, width = k_ref.shape
    h = _rms_norm(x_ref[...], g_ref[...], RMS_EPS).astype(BF16)
    z = jnp.dot(h, w_ref[...], preferred_element_type=F32)
    tabs = [rope_ref[:, t * LANES:(t + 1) * LANES] for t in range(6)]

    def rope(zs, cos, sin_lo, sin_hi):
        up = pltpu.roll(zs, LANES - HEAD_DIM // 2, 1)
        down = pltpu.roll(zs, HEAD_DIM // 2, 1)
        return zs * cos + up * sin_lo + down * sin_hi

    def store_transposed(dst_ref, s, val):
        for t in range(tm // TQ):
            dst_ref[t, s * LANES:(s + 1) * LANES, :] = val[t * TQ:(t + 1) * TQ, :].T.astype(dst_ref.dtype)

    for s in range(width // LANES):
        lanes = slice(s * LANES, (s + 1) * LANES)
        store_transposed(qt_ref, s, rope(z[:, lanes], *tabs[:3]))
        k_ref[:, lanes] = rope(z[:, width + s * LANES: width + (s + 1) * LANES],
                               *tabs[3:]).astype(k_ref.dtype)
        store_transposed(vt_ref, s, z[:, 2 * width + s * LANES: 2 * width + (s + 1) * LANES])
    a = z[:, 3 * width:4 * width]
    b = z[:, 4 * width:5 * width]
    c_ref[...] = (a * jax.nn.sigmoid(b)).astype(c_ref.dtype)


def _inproj(x2d, g, w, rope_tab, seq):
    tokens, d_model = x2d.shape
    width = N_HEADS * V_HEAD_DIM
    tm = TM_PROJ
    tiles_per_seq = seq // tm
    row = lambda i: (i, 0)
    pos = lambda i: (i % tiles_per_seq, 0)
    nat = jax.ShapeDtypeStruct((tokens, width), BF16)
    tr = jax.ShapeDtypeStruct((tokens // TQ, width, TQ), BF16)
    tr_spec = pl.BlockSpec((tm // TQ, width, TQ), lambda i: (i, 0, 0))
    return pl.pallas_call(
        _inproj_kernel,
        grid=(tokens // tm,),
        in_specs=[
            pl.BlockSpec((tm, d_model), row),
            _const_spec((1, d_model)),
            _const_spec(w.shape),
            pl.BlockSpec((tm, rope_tab.shape[1]), pos),
        ],
        out_specs=[tr_spec, pl.BlockSpec((tm, width), row), tr_spec, pl.BlockSpec((tm, width), row)],
        out_shape=[tr, nat, tr, nat],
        compiler_params=pltpu.CompilerParams(
            dimension_semantics=("parallel",), vmem_limit_bytes=VMEM_LIMIT),
        name="inproj",
    )(x2d, g, w, rope_tab)


def _attn_kernel(lq1_ref, lk1_ref, lq2_ref, lk2_ref, gsub_ref, qt_ref, k_ref, vt_ref,
                 o_ref, acc_ref):
    nq = qt_ref.shape[0]
    tq = TQ
    lam = (jnp.exp(jnp.sum(lq1_ref[...] * lk1_ref[...], axis=-1, keepdims=True))
           - jnp.exp(jnp.sum(lq2_ref[...] * lk2_ref[...], axis=-1, keepdims=True))
           + LAM_INIT)
    gsub = gsub_ref[...] * (1.0 - LAM_INIT)

    key_chunk = lax.broadcasted_iota(jnp.int32, (tq, 2 * tq), 0) // CHUNK
    qry_chunk = (lax.broadcasted_iota(jnp.int32, (tq, 2 * tq), 1) & (tq - 1)) // CHUNK
    diag_mask = key_chunk <= qry_chunk
    zeros_half = jnp.zeros((HEAD_DIM, tq), BF16)

    def feats(h):
        return slice(h * V_HEAD_DIM, (h + 1) * V_HEAD_DIM)

    def scores(h, qs, j):
        kb = k_ref[pl.ds(pl.multiple_of(j * tq, tq), tq), feats(h)]
        return jnp.dot(kb, qs, preferred_element_type=F32)

    ones_rows = jnp.ones((DENOM_ROWS, tq), BF16)

    def accumulate(h, s, j, m, mask):
        if mask is not None:
            s = jnp.where(mask, s, NEG_BIG)
        m_new = jnp.maximum(m, jnp.max(s, axis=0, keepdims=True))
        alpha = jnp.exp2(m - m_new)
        p = jnp.exp2(s - m_new).astype(BF16)
        vb = jnp.concatenate([vt_ref[j, feats(h), :], ones_rows], axis=0)
        acc_ref[h] = alpha * acc_ref[h] + jnp.dot(vb, p, preferred_element_type=F32)
        return m_new

    def q_block(i, carry):
        qs = []
        for h in range(N_HEADS):
            qt = qt_ref[i, h * V_HEAD_DIM:(h + 1) * V_HEAD_DIM, :]
            qs.append(jnp.concatenate(
                [jnp.concatenate([qt[:HEAD_DIM], zeros_half], axis=0),
                 jnp.concatenate([zeros_half, qt[HEAD_DIM:]], axis=0)], axis=1))
        acc_ref[...] = jnp.zeros_like(acc_ref)
        m0 = jnp.full((1, 2 * tq), NEG_BIG, F32)

        def k_block(j, m, mask=None):
            s = [scores(h, qs[h], j) for h in range(N_HEADS)]
            return tuple(accumulate(h, s[h], j, m[h], mask) for h in range(N_HEADS))

        m = lax.fori_loop(0, i, k_block, (m0,) * N_HEADS)
        k_block(i, m, diag_mask)
        for h in range(N_HEADS):
            acc = acc_ref[h]
            o = acc[:V_HEAD_DIM] / acc[V_HEAD_DIM:V_HEAD_DIM + 1]
            o = o[:, :tq] - lam * o[:, tq:]
            o = o * lax.rsqrt(jnp.mean(o * o, axis=0, keepdims=True) + SUB_EPS)
            o_ref[pl.ds(pl.multiple_of(i * tq, tq), tq), h * V_HEAD_DIM:(h + 1) * V_HEAD_DIM] = (
                (o * gsub).T.astype(o_ref.dtype))
        return carry

    lax.fori_loop(0, nq, q_block, 0)


def _attention(qt, k, vt, seq, lq1, lk1, lq2, lk2, g_sub):
    tokens, width = k.shape
    nq = seq // TQ
    tr_spec = pl.BlockSpec((nq, width, TQ), lambda b: (b, 0, 0))
    nat_spec = pl.BlockSpec((seq, width), lambda b: (b, 0))
    vec = lambda shape: pl.BlockSpec(shape, lambda b: (0, 0))
    return pl.pallas_call(
        _attn_kernel,
        grid=(tokens // seq,),
        in_specs=[vec((1, HEAD_DIM))] * 4 + [vec((V_HEAD_DIM, 1)), tr_spec, nat_spec, tr_spec],
        out_specs=nat_spec,
        out_shape=jax.ShapeDtypeStruct((tokens, width), BF16),
        scratch_shapes=[pltpu.VMEM((N_HEADS, V_HEAD_DIM + DENOM_ROWS, 2 * TQ), F32)],
        compiler_params=pltpu.CompilerParams(
            dimension_semantics=("parallel",), vmem_limit_bytes=VMEM_LIMIT),
        name="diff_attn",
    )(lq1, lk1, lq2, lk2, g_sub, qt, k, vt)


def _mix_kernel(tiles_per_seq, x_ref, attn_ref, c_ref, halo_ref, g_pre_ref, w_gate_ref,
                b_gate_ref, w_dw_ref, b_dw_ref, ln_g_ref, ln_b_ref, w_attn_o_ref,
                w_conv_o_ref, w_out_ref, g_post_ref, o_ref, cbuf_ref):
    tm, d_model = x_ref.shape
    first = (pl.program_id(0) % tiles_per_seq) == 0
    halo = halo_ref[...].astype(F32)
    cbuf_ref[0:HALO, :] = jnp.where(first, jnp.zeros_like(halo), halo)
    cbuf_ref[HALO:, :] = c_ref[...].astype(F32)

    off = HALO - (CONV_K - 1)
    w_dw = w_dw_ref[...]
    conv = cbuf_ref[pl.ds(off, tm), :] * w_dw[0:1, :] + b_dw_ref[...]
    for j in range(1, CONV_K):
        conv = conv + cbuf_ref[pl.ds(off + j, tm), :] * w_dw[j:j + 1, :]

    mu = jnp.mean(conv, axis=-1, keepdims=True)
    cen = conv - mu
    var = jnp.mean(cen * cen, axis=-1, keepdims=True)
    y = cen * lax.rsqrt(var + LN_EPS) * ln_g_ref[...] + ln_b_ref[...]
    y = (y * jax.nn.sigmoid(y)).astype(BF16)
    y_c = jnp.dot(y, w_conv_o_ref[...], preferred_element_type=F32)
    y_a = jnp.dot(attn_ref[...], w_attn_o_ref[...], preferred_element_type=F32)

    x = x_ref[...]
    h = _rms_norm(x, g_pre_ref[...], RMS_EPS).astype(BF16)
    gate = jnp.dot(h, w_gate_ref[...], preferred_element_type=F32) + b_gate_ref[...]
    gate = jax.nn.sigmoid(gate)
    m = (gate[:, :d_model] * y_a + gate[:, d_model:] * y_c).astype(BF16)
    o = jnp.dot(m, w_out_ref[...], preferred_element_type=F32)
    o_ref[...] = x + _rms_norm(o, g_post_ref[...], RMS_EPS)


def _mix(x2d, attn2d, c2d, seq, g_pre, w_gate, b_gate, w_dw, b_dw, ln_g, ln_b,
         w_attn_o, w_conv_o, w_out, g_post):
    tokens, d_model = x2d.shape
    width = attn2d.shape[1]
    tm = TM_MIX
    tiles_per_seq = seq // tm
    row = lambda i: (i, 0)
    halo = lambda i: (jnp.maximum(i * (tm // HALO) - 1, 0), 0)
    consts = [g_pre, w_gate, b_gate, w_dw, b_dw, ln_g, ln_b, w_attn_o, w_conv_o, w_out, g_post]
    return pl.pallas_call(
        functools.partial(_mix_kernel, tiles_per_seq),
        grid=(tokens // tm,),
        in_specs=[
            pl.BlockSpec((tm, d_model), row),
            pl.BlockSpec((tm, width), row),
            pl.BlockSpec((tm, width), row),
            pl.BlockSpec((HALO, width), halo),
        ] + [_const_spec(c.shape) for c in consts],
        out_specs=pl.BlockSpec((tm, d_model), row),
        out_shape=jax.ShapeDtypeStruct((tokens, d_model), F32),
        scratch_shapes=[pltpu.VMEM((tm + HALO, width), F32)],
        compiler_params=pltpu.CompilerParams(
            dimension_semantics=("parallel",), vmem_limit_bytes=VMEM_LIMIT),
        name="mix",
    )(x2d, attn2d, c2d, c2d, *consts)


def _ffn_kernel(tiles_per_seq, x_ref, g_pre_ref, w_a_ref, w_b_ref, w_fc_ref, b_fc_ref,
                w_down_ref, g_post_ref, o_ref, abuf_ref):
    tm = x_ref.shape[0]
    first = (pl.program_id(0) % tiles_per_seq) == 0

    @pl.when(first)
    def _():
        abuf_ref[0:SUBLANES, :] = jnp.zeros((SUBLANES, abuf_ref.shape[1]), F32)

    x = x_ref[...]
    h = _rms_norm(x, g_pre_ref[...], RMS_EPS).astype(BF16)
    abuf_ref[SUBLANES:, :] = jnp.dot(h, w_a_ref[...], preferred_element_type=F32)
    b = jnp.dot(h, w_b_ref[...], preferred_element_type=F32)
    w_fc = w_fc_ref[...]
    off = SUBLANES - (FFN_CONV_K - 1)
    a = b_fc_ref[...] + abuf_ref[pl.ds(off, tm), :] * w_fc[0:1, :]
    for j in range(1, FFN_CONV_K):
        a = a + abuf_ref[pl.ds(off + j, tm), :] * w_fc[j:j + 1, :]
    abuf_ref[0:SUBLANES, :] = abuf_ref[tm:tm + SUBLANES, :]
    p = (jax.nn.gelu(a, approximate=True) * b).astype(BF16)
    f = jnp.dot(p, w_down_ref[...], preferred_element_type=F32)
    o_ref[...] = x + _rms_norm(f, g_post_ref[...], RMS_EPS)


def _ffn(x2d, seq, g_pre, w_a, w_b, w_fc, b_fc, w_down, g_post):
    tokens, d_model = x2d.shape
    ffn = w_a.shape[1]
    tm = TM_FFN
    tiles_per_seq = seq // tm
    row = lambda i: (i, 0)
    consts = [g_pre, w_a, w_b, w_fc, b_fc, w_down, g_post]
    return pl.pallas_call(
        functools.partial(_ffn_kernel, tiles_per_seq),
        grid=(tokens // tm,),
        in_specs=[pl.BlockSpec((tm, d_model), row)] + [_const_spec(c.shape) for c in consts],
        out_specs=pl.BlockSpec((tm, d_model), row),
        out_shape=jax.ShapeDtypeStruct((tokens, d_model), F32),
        scratch_shapes=[pltpu.VMEM((tm + SUBLANES, ffn), F32)],
        compiler_params=pltpu.CompilerParams(
            dimension_semantics=("arbitrary",), vmem_limit_bytes=VMEM_LIMIT),
        name="conv_ffn",
    )(x2d, *consts)


def _rope_tables(seq):
    half = HEAD_DIM // 2
    inv_freq = ROPE_THETA ** (-jnp.arange(0, HEAD_DIM, 2, dtype=F32) / HEAD_DIM)
    ang = jnp.arange(seq, dtype=jnp.int32).astype(F32)[:, None] * inv_freq[None, :]
    ang = jnp.concatenate([ang, ang], axis=-1)
    cos, sin = jnp.cos(ang), jnp.sin(ang)
    lo = (jnp.arange(HEAD_DIM) < half)[None, :]
    sin_lo = jnp.where(lo, -sin, 0.0)
    sin_hi = jnp.where(lo, 0.0, sin)
    rep = LANES // HEAD_DIM
    k_tabs = [jnp.tile(t, (1, rep)) for t in (cos, sin_lo, sin_hi)]
    q_scale = HEAD_DIM ** -0.5 * math.log2(math.e)
    return jnp.concatenate([t * q_scale for t in k_tabs] + k_tabs, axis=1)


def kernel(x, g_mix_pre, w_in, lambda_q1, lambda_k1, lambda_q2, lambda_k2, g_sub,
           w_attn_o, w_dw, b_dw, ln_g, ln_b, w_conv_o, b_gate, w_out, g_mix_post,
           g_ffn_pre, w_up, w_fconv, b_fconv, w_down, g_ffn_post):
    bsz, seq, d_model = x.shape
    depth = w_in.shape[0]
    assert depth == 1, "LAM_INIT is specialised to a single layer"
    assert seq % TM_PROJ == 0 and seq % TM_MIX == 0 and seq % TM_FFN == 0 and seq % TQ == 0
    width = N_HEADS * V_HEAD_DIM
    ffn = w_fconv.shape[-1]
    rope_tab = _rope_tables(seq)
    x2d = x.reshape(bsz * seq, d_model)
    row2 = lambda a: a.reshape(1, -1)
    l = 0

    w_proj = w_in[l][:, :5 * width].astype(BF16)
    w_gate = w_in[l][:, 5 * width:].astype(BF16)

    qt, k, vt, c = _inproj(x2d, row2(g_mix_pre[l]), w_proj, rope_tab, seq)
    attn = _attention(qt, k, vt, seq, row2(lambda_q1[l]), row2(lambda_k1[l]), row2(lambda_q2[l]),
                      row2(lambda_k2[l]), g_sub[l].reshape(-1, 1))
    w_dw_pad = jnp.pad(w_dw[l], ((0, HALO - CONV_K), (0, 0)))
    x2d = _mix(x2d, attn, c, seq, row2(g_mix_pre[l]), w_gate,
               row2(b_gate[l]), w_dw_pad, row2(b_dw[l]), row2(ln_g[l]), row2(ln_b[l]),
               w_attn_o[l].astype(BF16), w_conv_o[l].astype(BF16), w_out[l].astype(BF16),
               row2(g_mix_post[l]))
    w_fc_pad = jnp.pad(w_fconv[l], ((0, SUBLANES - FFN_CONV_K), (0, 0)))
    x2d = _ffn(x2d, seq, row2(g_ffn_pre[l]), w_up[l][:, :ffn].astype(BF16),
               w_up[l][:, ffn:].astype(BF16), w_fc_pad, row2(b_fconv[l]),
               w_down[l].astype(BF16), row2(g_ffn_post[l]))
    return x2d.reshape(bsz, seq, d_model)
```

```python
import functools
import math

import jax
import jax.numpy as jnp
from jax import lax
from jax.experimental import pallas as pl
from jax.experimental.pallas import tpu as pltpu

N_HEADS = 4
HEAD_DIM = 64
V_HEAD_DIM = 2 * HEAD_DIM
CHUNK = 64
CONV_K = 31
FFN_CONV_K = 3
ROPE_THETA = 10000.0
RMS_EPS = 1e-6
SUB_EPS = 1e-5
LN_EPS = 1e-5
LAM_INIT = 0.8 - 0.6 * math.exp(-0.3 * 0)

LANES = 128
SUBLANES = 8
VMEM_LIMIT = 56 * 1024 * 1024

TM_PROJ = 512
TM_FFN = 256
TQ = 256
DENOM_ROWS = 16
HALO = 32
CONV_ROWS = 64
NEG_BIG = -1e30

BF16 = jnp.bfloat16
F32 = jnp.float32


def _rms_norm(x, g, eps):
    ms = jnp.mean(x * x, axis=-1, keepdims=True)
    return (x * lax.rsqrt(ms + eps)) * g


def _const_spec(shape):
    nd = len(shape)
    return pl.BlockSpec(shape, lambda *_: (0,) * nd, pipeline_mode=pl.Buffered(1))


def _inproj_kernel(x_ref, g_ref, w_ref, rope_ref, qt_ref, k_ref, vt_ref, c_ref):
    tm, width = k_ref.shape
    h = _rms_norm(x_ref[...], g_ref[...], RMS_EPS).astype(BF16)
    z = jnp.dot(h, w_ref[...], preferred_element_type=F32)
    tabs = [rope_ref[:, t * LANES:(t + 1) * LANES] for t in range(6)]

    def rope(zs, cos, sin_lo, sin_hi):
        up = pltpu.roll(zs, LANES - HEAD_DIM // 2, 1)
        down = pltpu.roll(zs, HEAD_DIM // 2, 1)
        return zs * cos + up * sin_lo + down * sin_hi

    def store_transposed(dst_ref, s, val):
        for t in range(tm // TQ):
            dst_ref[t, s * LANES:(s + 1) * LANES, :] = val[t * TQ:(t + 1) * TQ, :].T.astype(dst_ref.dtype)

    for s in range(width // LANES):
        lanes = slice(s * LANES, (s + 1) * LANES)
        store_transposed(qt_ref, s, rope(z[:, lanes], *tabs[:3]))
        k_ref[:, lanes] = rope(z[:, width + s * LANES: width + (s + 1) * LANES],
                               *tabs[3:]).astype(k_ref.dtype)
        store_transposed(vt_ref, s, z[:, 2 * width + s * LANES: 2 * width + (s + 1) * LANES])

    a = z[:, 3 * width:4 * width]
    b = z[:, 4 * width:5 * width]
    c_ref[...] = (a * jax.nn.sigmoid(b)).astype(c_ref.dtype)


def _inproj(x2d, g, w, rope_tab, seq):
    tokens, d_model = x2d.shape
    width = N_HEADS * V_HEAD_DIM
    tm = TM_PROJ
    tiles_per_seq = seq // tm
    row = lambda i: (i, 0)
    pos = lambda i: (i % tiles_per_seq, 0)
    nat = jax.ShapeDtypeStruct((tokens, width), BF16)
    tr = jax.ShapeDtypeStruct((tokens // TQ, width, TQ), BF16)
    tr_spec = pl.BlockSpec((tm // TQ, width, TQ), lambda i: (i, 0, 0))
    return pl.pallas_call(
        _inproj_kernel,
        grid=(tokens // tm,),
        in_specs=[
            pl.BlockSpec((tm, d_model), row),
            _const_spec((1, d_model)),
            _const_spec(w.shape),
            pl.BlockSpec((tm, rope_tab.shape[1]), pos),
        ],
        out_specs=[tr_spec, pl.BlockSpec((tm, width), row), tr_spec, pl.BlockSpec((tm, width), row)],
        out_shape=[tr, nat, tr, nat],
        compiler_params=pltpu.CompilerParams(
            dimension_semantics=("parallel",), vmem_limit_bytes=VMEM_LIMIT),
        name="inproj",
    )(x2d, g, w, rope_tab)


def _attn_kernel(lq1_ref, lk1_ref, lq2_ref, lk2_ref, gsub_ref, qt_ref, k_ref, vt_ref,
                 o_ref, acc_ref):
    nq = qt_ref.shape[0]
    tq = TQ
    lam = (jnp.exp(jnp.sum(lq1_ref[...] * lk1_ref[...], axis=-1, keepdims=True))
           - jnp.exp(jnp.sum(lq2_ref[...] * lk2_ref[...], axis=-1, keepdims=True))
           + LAM_INIT)
    gsub = gsub_ref[...] * (1.0 - LAM_INIT)

    key_chunk = lax.broadcasted_iota(jnp.int32, (tq, 2 * tq), 0) // CHUNK
    qry_chunk = (lax.broadcasted_iota(jnp.int32, (tq, 2 * tq), 1) & (tq - 1)) // CHUNK
    diag_mask = key_chunk <= qry_chunk
    zeros_half = jnp.zeros((HEAD_DIM, tq), BF16)

    def feats(h):
        return slice(h * V_HEAD_DIM, (h + 1) * V_HEAD_DIM)

    def scores(h, qs, j):
        kb = k_ref[pl.ds(pl.multiple_of(j * tq, tq), tq), feats(h)]
        return jnp.dot(kb, qs, preferred_element_type=F32)

    ones_rows = jnp.ones((DENOM_ROWS, tq), BF16)

    def accumulate(h, s, j, m, mask):
        if mask is not None:
            s = jnp.where(mask, s, NEG_BIG)
        m_new = jnp.maximum(m, jnp.max(s, axis=0, keepdims=True))
        alpha = jnp.exp2(m - m_new)
        p = jnp.exp2(s - m_new).astype(BF16)
        vb = jnp.concatenate([vt_ref[j, feats(h), :], ones_rows], axis=0)
        acc_ref[h] = alpha * acc_ref[h] + jnp.dot(vb, p, preferred_element_type=F32)
        return m_new

    def q_block(i, carry):
        qs = []
        for h in range(N_HEADS):
            qt = qt_ref[i, h * V_HEAD_DIM:(h + 1) * V_HEAD_DIM, :]
            qs.append(jnp.concatenate(
                [jnp.concatenate([qt[:HEAD_DIM], zeros_half], axis=0),
                 jnp.concatenate([zeros_half, qt[HEAD_DIM:]], axis=0)], axis=1))
        acc_ref[...] = jnp.zeros_like(acc_ref)
        m0 = jnp.full((1, 2 * tq), NEG_BIG, F32)

        def k_block(j, m, mask=None):
            s = [scores(h, qs[h], j) for h in range(N_HEADS)]
            return tuple(accumulate(h, s[h], j, m[h], mask) for h in range(N_HEADS))

        m = lax.fori_loop(0, i, k_block, (m0,) * N_HEADS)
        k_block(i, m, diag_mask)
        for h in range(N_HEADS):
            acc = acc_ref[h]
            o = acc[:V_HEAD_DIM] / acc[V_HEAD_DIM:V_HEAD_DIM + 1]
            o = o[:, :tq] - lam * o[:, tq:]
            o = o * lax.rsqrt(jnp.mean(o * o, axis=0, keepdims=True) + SUB_EPS)
            o_ref[pl.ds(pl.multiple_of(i * tq, tq), tq), h * V_HEAD_DIM:(h + 1) * V_HEAD_DIM] = (
                (o * gsub).T.astype(o_ref.dtype))
        return carry

    lax.fori_loop(0, nq, q_block, 0)


def _attention(qt, k, vt, seq, lq1, lk1, lq2, lk2, g_sub):
    tokens, width = k.shape
    nq = seq // TQ
    tr_spec = pl.BlockSpec((nq, width, TQ), lambda b: (b, 0, 0))
    nat_spec = pl.BlockSpec((seq, width), lambda b: (b, 0))
    vec = lambda shape: pl.BlockSpec(shape, lambda b: (0, 0))
    return pl.pallas_call(
        _attn_kernel,
        grid=(tokens // seq,),
        in_specs=[vec((1, HEAD_DIM))] * 4 + [vec((V_HEAD_DIM, 1)), tr_spec, nat_spec, tr_spec],
        out_specs=nat_spec,
        out_shape=jax.ShapeDtypeStruct((tokens, width), BF16),
        scratch_shapes=[pltpu.VMEM((N_HEADS, V_HEAD_DIM + DENOM_ROWS, 2 * TQ), F32)],
        compiler_params=pltpu.CompilerParams(
            dimension_semantics=("parallel",), vmem_limit_bytes=VMEM_LIMIT),
        name="diff_attn",
    )(lq1, lk1, lq2, lk2, g_sub, qt, k, vt)


def _stage_shifted_rows(cbuf_ref, shift_ref):
    n8, _, width = shift_ref.shape[1:]
    for r in range(1, SUBLANES):
        shift_ref[r - 1] = cbuf_ref[pl.ds(r, n8 * SUBLANES), :].reshape(n8, SUBLANES, width)


def _depthwise_conv_rows(cbuf_ref, shift_ref, w_ref, b_ref, blk):
    width = cbuf_ref.shape[1]
    off = HALO - (CONV_K - 1)
    g8 = CONV_ROWS // SUBLANES
    acc = jnp.broadcast_to(b_ref[...][None], (g8, SUBLANES, width))
    for j in range(CONV_K):
        r, q8 = (off + j) % SUBLANES, (off + j) // SUBLANES
        if r == 0:
            tap = cbuf_ref[pl.ds(blk * CONV_ROWS + q8 * SUBLANES, CONV_ROWS), :].reshape(
                g8, SUBLANES, width)
        else:
            tap = shift_ref[r - 1, pl.ds(blk * g8 + q8, g8)]
        acc = acc + tap * w_ref[j]
    return acc.reshape(CONV_ROWS, width)


def _merge_ffn_kernel(tiles_per_seq, x_ref, attn_ref, c_ref, halo_ref, w_dw_ref, b_dw_ref,
                      g_pre_ref, w_gate_ref, b_gate_ref, ln_g_ref, ln_b_ref, w_attn_o_ref,
                      w_conv_o_ref, w_out_ref, g_post_ref, g_ffn_pre_ref, w_a_ref, w_b_ref,
                      w_fc_ref, b_fc_ref, w_down_ref, g_ffn_post_ref, o_ref,
                      abuf_ref, cbuf_ref, shift_ref):
    tm, d_model = x_ref.shape
    ffn = abuf_ref.shape[1]
    first = pl.program_id(0) % tiles_per_seq == 0

    @pl.when(first)
    def _():
        abuf_ref[0:SUBLANES, :] = jnp.zeros((SUBLANES, ffn), F32)

    x = x_ref[...]
    h = _rms_norm(x, g_pre_ref[...], RMS_EPS).astype(BF16)
    gate = jnp.dot(h, w_gate_ref[...], preferred_element_type=F32) + b_gate_ref[...]
    y_a = jnp.dot(attn_ref[...], w_attn_o_ref[...], preferred_element_type=F32)

    halo = halo_ref[...].astype(F32)
    cbuf_ref[0:HALO, :] = jnp.where(first, jnp.zeros_like(halo), halo)
    cbuf_ref[HALO:, :] = c_ref[...].astype(F32)
    _stage_shifted_rows(cbuf_ref, shift_ref)
    conv = jnp.concatenate(
        [_depthwise_conv_rows(cbuf_ref, shift_ref, w_dw_ref, b_dw_ref, blk)
         for blk in range(tm // CONV_ROWS)], axis=0)
    mu = jnp.mean(conv, axis=-1, keepdims=True)
    cen = conv - mu
    var = jnp.mean(cen * cen, axis=-1, keepdims=True)
    y = cen * lax.rsqrt(var + LN_EPS) * ln_g_ref[...] + ln_b_ref[...]
    y = (y * jax.nn.sigmoid(y)).astype(BF16)
    y_c = jnp.dot(y, w_conv_o_ref[...], preferred_element_type=F32)

    gate = jax.nn.sigmoid(gate)
    m = (gate[:, :d_model] * y_a + gate[:, d_model:] * y_c).astype(BF16)
    o = jnp.dot(m, w_out_ref[...], preferred_element_type=F32)
    x = x + _rms_norm(o, g_post_ref[...], RMS_EPS)

    h = _rms_norm(x, g_ffn_pre_ref[...], RMS_EPS).astype(BF16)
    abuf_ref[SUBLANES:, :] = jnp.dot(h, w_a_ref[...], preferred_element_type=F32)
    b = jnp.dot(h, w_b_ref[...], preferred_element_type=F32)
    w_fc = w_fc_ref[...]
    off = SUBLANES - (FFN_CONV_K - 1)
    a = b_fc_ref[...] + abuf_ref[pl.ds(off, tm), :] * w_fc[0:1, :]
    for j in range(1, FFN_CONV_K):
        a = a + abuf_ref[pl.ds(off + j, tm), :] * w_fc[j:j + 1, :]
    abuf_ref[0:SUBLANES, :] = abuf_ref[tm:tm + SUBLANES, :]
    p = (jax.nn.gelu(a, approximate=True) * b).astype(BF16)
    f = jnp.dot(p, w_down_ref[...], preferred_element_type=F32)
    o_ref[...] = x + _rms_norm(f, g_ffn_post_ref[...], RMS_EPS)


def _merge_ffn(x2d, attn2d, c2d, seq, consts):
    tokens, d_model = x2d.shape
    width = attn2d.shape[1]
    ffn = consts[-2].shape[0]
    tm = TM_FFN
    tiles_per_seq = seq // tm
    row = lambda i: (i, 0)
    halo = lambda i: (jnp.maximum(i * (tm // HALO) - 1, 0), 0)
    n8 = (tm + HALO) // SUBLANES - 1
    return pl.pallas_call(
        functools.partial(_merge_ffn_kernel, tiles_per_seq),
        grid=(tokens // tm,),
        in_specs=[
            pl.BlockSpec((tm, d_model), row),
            pl.BlockSpec((tm, width), row),
            pl.BlockSpec((tm, width), row),
            pl.BlockSpec((HALO, width), halo),
        ] + [_const_spec(c.shape) for c in consts],
        out_specs=pl.BlockSpec((tm, d_model), row),
        out_shape=jax.ShapeDtypeStruct((tokens, d_model), F32),
        scratch_shapes=[pltpu.VMEM((tm + SUBLANES, ffn), F32),
                        pltpu.VMEM((tm + HALO, width), F32),
                        pltpu.VMEM((SUBLANES - 1, n8, SUBLANES, width), F32)],
        compiler_params=pltpu.CompilerParams(
            dimension_semantics=("arbitrary",), vmem_limit_bytes=VMEM_LIMIT),
        name="merge_ffn",
    )(x2d, attn2d, c2d, c2d, *consts)


def _rope_tables(seq):
    half = HEAD_DIM // 2
    inv_freq = ROPE_THETA ** (-jnp.arange(0, HEAD_DIM, 2, dtype=F32) / HEAD_DIM)
    ang = jnp.arange(seq, dtype=jnp.int32).astype(F32)[:, None] * inv_freq[None, :]
    ang = jnp.concatenate([ang, ang], axis=-1)
    cos, sin = jnp.cos(ang), jnp.sin(ang)
    lo = (jnp.arange(HEAD_DIM) < half)[None, :]
    sin_lo = jnp.where(lo, -sin, 0.0)
    sin_hi = jnp.where(lo, 0.0, sin)
    rep = LANES // HEAD_DIM
    k_tabs = [jnp.tile(t, (1, rep)) for t in (cos, sin_lo, sin_hi)]
    q_scale = HEAD_DIM ** -0.5 * math.log2(math.e)
    return jnp.concatenate([t * q_scale for t in k_tabs] + k_tabs, axis=1)


def kernel(x, g_mix_pre, w_in, lambda_q1, lambda_k1, lambda_q2, lambda_k2, g_sub,
           w_attn_o, w_dw, b_dw, ln_g, ln_b, w_conv_o, b_gate, w_out, g_mix_post,
           g_ffn_pre, w_up, w_fconv, b_fconv, w_down, g_ffn_post):
    bsz, seq, d_model = x.shape
    depth = w_in.shape[0]
    assert depth == 1, "LAM_INIT is specialised to a single layer"
    assert seq % TM_PROJ == 0 and seq % TM_FFN == 0 and seq % TQ == 0
    width = N_HEADS * V_HEAD_DIM
    ffn = w_fconv.shape[-1]
    rope_tab = _rope_tables(seq)
    x2d = x.reshape(bsz * seq, d_model)
    row2 = lambda a: a.reshape(1, -1)
    rows8 = lambda a: jnp.broadcast_to(a[..., None, :], a.shape[:-1] + (SUBLANES, a.shape[-1]))
    l = 0

    w_proj = w_in[l][:, :5 * width].astype(BF16)
    w_gate = w_in[l][:, 5 * width:].astype(BF16)

    qt, k, vt, c = _inproj(x2d, row2(g_mix_pre[l]), w_proj, rope_tab, seq)
    attn = _attention(qt, k, vt, seq, row2(lambda_q1[l]), row2(lambda_k1[l]), row2(lambda_q2[l]),
                      row2(lambda_k2[l]), g_sub[l].reshape(-1, 1))
    w_fc_pad = jnp.pad(w_fconv[l], ((0, SUBLANES - FFN_CONV_K), (0, 0)))
    consts = [rows8(w_dw[l]), rows8(b_dw[l]), row2(g_mix_pre[l]), w_gate, row2(b_gate[l]), row2(ln_g[l]), row2(ln_b[l]),
              w_attn_o[l].astype(BF16), w_conv_o[l].astype(BF16), w_out[l].astype(BF16),
              row2(g_mix_post[l]), row2(g_ffn_pre[l]), w_up[l][:, :ffn].astype(BF16),
              w_up[l][:, ffn:].astype(BF16), w_fc_pad, row2(b_fconv[l]), w_down[l].astype(BF16),
              row2(g_ffn_post[l])]
    x2d = _merge_ffn(x2d, attn, c, seq, consts)
    return x2d.reshape(bsz, seq, d_model)
```

```python
import functools
import math

import jax
import jax.numpy as jnp
from jax import lax
from jax.experimental import pallas as pl
from jax.experimental.pallas import tpu as pltpu

N_HEADS = 4
HEAD_DIM = 64
V_HEAD_DIM = 2 * HEAD_DIM
CHUNK = 64
CONV_K = 31
FFN_CONV_K = 3
ROPE_THETA = 10000.0
RMS_EPS = 1e-6
SUB_EPS = 1e-5
LN_EPS = 1e-5
LAM_INIT = 0.8 - 0.6 * math.exp(-0.3 * 0)
LANES = 128
SUBLANES = 8
VMEM_LIMIT = 56 * 1024 * 1024

TM_PROJ = 512
TM_FFN = 256
TQ = 256
DENOM_ROWS = 16
HALO = 32
CONV_ROWS = 64
NEG_BIG = -1e30

BF16 = jnp.bfloat16
F32 = jnp.float32


def _rms_norm(x, g, eps):
    ms = jnp.mean(x * x, axis=-1, keepdims=True)
    return (x * lax.rsqrt(ms + eps)) * g


def _const_spec(shape):
    nd = len(shape)
    return pl.BlockSpec(shape, lambda *_: (0,) * nd, pipeline_mode=pl.Buffered(1))


def _inproj_kernel(x_ref, g_ref, w_ref, rope_ref, qt_ref, k_ref, vt_ref, c_ref):
    tm, width = k_ref.shape
    h = _rms_norm(x_ref[...], g_ref[...], RMS_EPS).astype(BF16)
    z = jnp.dot(h, w_ref[...], preferred_element_type=F32)
    tabs = [rope_ref[:, t * LANES:(t + 1) * LANES] for t in range(6)]

    def rope(zs, cos, sin_lo, sin_hi):
        up = pltpu.roll(zs, LANES - HEAD_DIM // 2, 1)
        down = pltpu.roll(zs, HEAD_DIM // 2, 1)
        return zs * cos + up * sin_lo + down * sin_hi

    def store_transposed(dst_ref, s, val):
        for t in range(tm // TQ):
            dst_ref[t, s * LANES:(s + 1) * LANES, :] = val[t * TQ:(t + 1) * TQ, :].T.astype(dst_ref.dtype)

    for s in range(width // LANES):
        lanes = slice(s * LANES, (s + 1) * LANES)
        store_transposed(qt_ref, s, rope(z[:, lanes], *tabs[:3]))
        k_ref[:, lanes] = rope(z[:, width + s * LANES: width + (s + 1) * LANES],
                               *tabs[3:]).astype(k_ref.dtype)
        store_transposed(vt_ref, s, z[:, 2 * width + s * LANES: 2 * width + (s + 1) * LANES])

    a = z[:, 3 * width:4 * width]
    b = z[:, 4 * width:5 * width]
    c_ref[...] = (a * jax.nn.sigmoid(b)).astype(c_ref.dtype)


def _inproj(x2d, g, w, rope_tab, seq):
    tokens, d_model = x2d.shape
    width = N_HEADS * V_HEAD_DIM
    tm = TM_PROJ
    tiles_per_seq = seq // tm
    row = lambda i: (i, 0)
    pos = lambda i: (i % tiles_per_seq, 0)
    nat = jax.ShapeDtypeStruct((tokens, width), BF16)
    tr = jax.ShapeDtypeStruct((tokens // TQ, width, TQ), BF16)
    tr_spec = pl.BlockSpec((tm // TQ, width, TQ), lambda i: (i, 0, 0))
    return pl.pallas_call(
        _inproj_kernel,
        grid=(tokens // tm,),
        in_specs=[
            pl.BlockSpec((tm, d_model), row),
            _const_spec((1, d_model)),
            _const_spec(w.shape),
            pl.BlockSpec((tm, rope_tab.shape[1]), pos),
        ],
        out_specs=[tr_spec, pl.BlockSpec((tm, width), row), tr_spec, pl.BlockSpec((tm, width), row)],
        out_shape=[tr, nat, tr, nat],
        compiler_params=pltpu.CompilerParams(
            dimension_semantics=("parallel",), vmem_limit_bytes=VMEM_LIMIT),
        name="inproj",
    )(x2d, g, w, rope_tab)


def _attn_kernel(lq1_ref, lk1_ref, lq2_ref, lk2_ref, gsub_ref, qt_ref, k_ref, vt_ref,
                 o_ref, acc_ref, s_ref):
    nq = qt_ref.shape[0]
    tq = TQ
    lam = (jnp.exp(jnp.sum(lq1_ref[...] * lk1_ref[...], axis=-1, keepdims=True))
           - jnp.exp(jnp.sum(lq2_ref[...] * lk2_ref[...], axis=-1, keepdims=True))
           + LAM_INIT)
    gsub = gsub_ref[...] * (1.0 - LAM_INIT)

    key_chunk = lax.broadcasted_iota(jnp.int32, (tq, 2 * tq), 0) // CHUNK
    qry_chunk = (lax.broadcasted_iota(jnp.int32, (tq, 2 * tq), 1) & (tq - 1)) // CHUNK
    diag_mask = key_chunk <= qry_chunk
    zeros_half = jnp.zeros((HEAD_DIM, tq), BF16)

    def feats(h):
        return slice(h * V_HEAD_DIM, (h + 1) * V_HEAD_DIM)

    def scores(h, qs, j):
        kb = k_ref[pl.ds(pl.multiple_of(j * tq, tq), tq), feats(h)]
        return jnp.dot(kb, qs, preferred_element_type=F32)

    ones_rows = jnp.ones((DENOM_ROWS, tq), BF16)

    def accumulate(h, s, j, m, mask):
        if mask is not None:
            s = jnp.where(mask, s, NEG_BIG)
        m_new = jnp.maximum(m, jnp.max(s, axis=0, keepdims=True))
        alpha = jnp.exp2(m - m_new)
        p = jnp.exp2(s - m_new).astype(BF16)
        vb = jnp.concatenate([vt_ref[j, feats(h), :], ones_rows], axis=0)
        acc_ref[h] = alpha * acc_ref[h] + jnp.dot(vb, p, preferred_element_type=F32)
        return m_new

    def stacked_queries(i):
        qs = []
        for h in range(N_HEADS):
            qt = qt_ref[i, feats(h), :]
            qs.append(jnp.concatenate(
                [jnp.concatenate([qt[:HEAD_DIM], zeros_half], axis=0),
                 jnp.concatenate([zeros_half, qt[HEAD_DIM:]], axis=0)], axis=1))
        return qs

    def first_scores(i):
        qs = stacked_queries(i)
        for h in range(N_HEADS):
            s_ref[h] = scores(h, qs[h], 0)

    def q_block(i, carry):
        qs = stacked_queries(i)
        acc_ref[...] = jnp.zeros_like(acc_ref)
        m0 = jnp.full((1, 2 * tq), NEG_BIG, F32)

        def k_block(j, m):
            s_next = [scores(h, qs[h], j + 1) for h in range(N_HEADS)]
            m_new = []
            for h in range(N_HEADS):
                m_new.append(accumulate(h, s_ref[h], j, m[h], None))
                s_ref[h] = s_next[h]
            return tuple(m_new)

        m = lax.fori_loop(0, i, k_block, (m0,) * N_HEADS)
        for h in range(N_HEADS):
            accumulate(h, s_ref[h], i, m[h], diag_mask)
        first_scores(jnp.minimum(i + 1, nq - 1))
        for h in range(N_HEADS):
            acc = acc_ref[h]
            o = acc[:V_HEAD_DIM] / acc[V_HEAD_DIM:V_HEAD_DIM + 1]
            o = o[:, :tq] - lam * o[:, tq:]
            o = o * lax.rsqrt(jnp.mean(o * o, axis=0, keepdims=True) + SUB_EPS)
            o_ref[pl.ds(pl.multiple_of(i * tq, tq), tq), feats(h)] = (o * gsub).T.astype(o_ref.dtype)
        return carry

    first_scores(0)
    lax.fori_loop(0, nq, q_block, 0)


def _attention(qt, k, vt, seq, lq1, lk1, lq2, lk2, g_sub):
    tokens, width = k.shape
    nq = seq // TQ
    tr_spec = pl.BlockSpec((nq, width, TQ), lambda b: (b, 0, 0))
    nat_spec = pl.BlockSpec((seq, width), lambda b: (b, 0))
    vec = lambda shape: pl.BlockSpec(shape, lambda b: (0, 0))
    return pl.pallas_call(
        _attn_kernel,
        grid=(tokens // seq,),
        in_specs=[vec((1, HEAD_DIM))] * 4 + [vec((V_HEAD_DIM, 1)), tr_spec, nat_spec, tr_spec],
        out_specs=nat_spec,
        out_shape=jax.ShapeDtypeStruct((tokens, width), BF16),
        scratch_shapes=[pltpu.VMEM((N_HEADS, V_HEAD_DIM + DENOM_ROWS, 2 * TQ), F32),
                        pltpu.VMEM((N_HEADS, TQ, 2 * TQ), F32)],
        compiler_params=pltpu.CompilerParams(
            dimension_semantics=("parallel",), vmem_limit_bytes=VMEM_LIMIT),
        name="diff_attn",
    )(lq1, lk1, lq2, lk2, g_sub, qt, k, vt)


def _stage_shifted_rows(cbuf_ref, shift_ref):
    n8, _, width = shift_ref.shape[1:]
    for r in range(1, SUBLANES):
        shift_ref[r - 1] = cbuf_ref[pl.ds(r, n8 * SUBLANES), :].reshape(n8, SUBLANES, width)


def _depthwise_conv_rows(cbuf_ref, shift_ref, w_ref, b_ref, blk):
    width = cbuf_ref.shape[1]
    off = HALO - (CONV_K - 1)
    g8 = CONV_ROWS // SUBLANES
    acc = jnp.broadcast_to(b_ref[...][None], (g8, SUBLANES, width))
    for j in range(CONV_K):
        r, q8 = (off + j) % SUBLANES, (off + j) // SUBLANES
        if r == 0:
            tap = cbuf_ref[pl.ds(blk * CONV_ROWS + q8 * SUBLANES, CONV_ROWS), :].reshape(
                g8, SUBLANES, width)
        else:
            tap = shift_ref[r - 1, pl.ds(blk * g8 + q8, g8)]
        acc = acc + tap * w_ref[j]
    return acc.reshape(CONV_ROWS, width)


def _merge_ffn_kernel(tiles_per_seq, x_ref, attn_ref, c_ref, halo_ref, w_dw_ref, b_dw_ref,
                      g_pre_ref, w_gate_ref, b_gate_ref, ln_g_ref, ln_b_ref, w_attn_o_ref,
                      w_conv_o_ref, w_out_ref, g_post_ref, g_ffn_pre_ref, w_a_ref, w_b_ref,
                      w_fc_ref, b_fc_ref, w_down_ref, g_ffn_post_ref, o_ref,
                      abuf_ref, cbuf_ref, shift_ref):
    tm, d_model = x_ref.shape
    ffn = abuf_ref.shape[1]
    first = pl.program_id(0) % tiles_per_seq == 0

    @pl.when(first)
    def _():
        abuf_ref[0:SUBLANES, :] = jnp.zeros((SUBLANES, ffn), F32)

    x = x_ref[...]
    h = _rms_norm(x, g_pre_ref[...], RMS_EPS).astype(BF16)
    gate = jnp.dot(h, w_gate_ref[...], preferred_element_type=F32) + b_gate_ref[...]
    y_a = jnp.dot(attn_ref[...], w_attn_o_ref[...], preferred_element_type=F32)

    halo = halo_ref[...].astype(F32)
    cbuf_ref[0:HALO, :] = jnp.where(first, jnp.zeros_like(halo), halo)
    cbuf_ref[HALO:, :] = c_ref[...].astype(F32)
    _stage_shifted_rows(cbuf_ref, shift_ref)
    conv = jnp.concatenate(
        [_depthwise_conv_rows(cbuf_ref, shift_ref, w_dw_ref, b_dw_ref, blk)
         for blk in range(tm // CONV_ROWS)], axis=0)
    mu = jnp.mean(conv, axis=-1, keepdims=True)
    cen = conv - mu
    var = jnp.mean(cen * cen, axis=-1, keepdims=True)
    y = cen * lax.rsqrt(var + LN_EPS) * ln_g_ref[...] + ln_b_ref[...]
    y = (y * jax.nn.sigmoid(y)).astype(BF16)
    y_c = jnp.dot(y, w_conv_o_ref[...], preferred_element_type=F32)

    gate = jax.nn.sigmoid(gate)
    m = (gate[:, :d_model] * y_a + gate[:, d_model:] * y_c).astype(BF16)
    o = jnp.dot(m, w_out_ref[...], preferred_element_type=F32)
    x = x + _rms_norm(o, g_post_ref[...], RMS_EPS)

    h = _rms_norm(x, g_ffn_pre_ref[...], RMS_EPS).astype(BF16)
    abuf_ref[SUBLANES:, :] = jnp.dot(h, w_a_ref[...], preferred_element_type=F32)
    b = jnp.dot(h, w_b_ref[...], preferred_element_type=F32)
    w_fc = w_fc_ref[...]
    off = SUBLANES - (FFN_CONV_K - 1)
    a = b_fc_ref[...] + abuf_ref[pl.ds(off, tm), :] * w_fc[0:1, :]
    for j in range(1, FFN_CONV_K):
        a = a + abuf_ref[pl.ds(off + j, tm), :] * w_fc[j:j + 1, :]
    abuf_ref[0:SUBLANES, :] = abuf_ref[tm:tm + SUBLANES, :]
    p = (jax.nn.gelu(a, approximate=True) * b).astype(BF16)
    f = jnp.dot(p, w_down_ref[...], preferred_element_type=F32)
    o_ref[...] = x + _rms_norm(f, g_ffn_post_ref[...], RMS_EPS)


def _merge_ffn(x2d, attn2d, c2d, seq, consts):
    tokens, d_model = x2d.shape
    width = attn2d.shape[1]
    ffn = consts[-2].shape[0]
    tm = TM_FFN
    tiles_per_seq = seq // tm
    row = lambda i: (i, 0)
    halo = lambda i: (jnp.maximum(i * (tm // HALO) - 1, 0), 0)
    n8 = (tm + HALO) // SUBLANES - 1
    return pl.pallas_call(
        functools.partial(_merge_ffn_kernel, tiles_per_seq),
        grid=(tokens // tm,),
        in_specs=[
            pl.BlockSpec((tm, d_model), row),
            pl.BlockSpec((tm, width), row),
            pl.BlockSpec((tm, width), row),
            pl.BlockSpec((HALO, width), halo),
        ] + [_const_spec(c.shape) for c in consts],
        out_specs=pl.BlockSpec((tm, d_model), row),
        out_shape=jax.ShapeDtypeStruct((tokens, d_model), F32),
        scratch_shapes=[pltpu.VMEM((tm + SUBLANES, ffn), F32),
                        pltpu.VMEM((tm + HALO, width), F32),
                        pltpu.VMEM((SUBLANES - 1, n8, SUBLANES, width), F32)],
        compiler_params=pltpu.CompilerParams(
            dimension_semantics=("arbitrary",), vmem_limit_bytes=VMEM_LIMIT),
        name="merge_ffn",
    )(x2d, attn2d, c2d, c2d, *consts)


def _rope_tables(seq):
    half = HEAD_DIM // 2
    inv_freq = ROPE_THETA ** (-jnp.arange(0, HEAD_DIM, 2, dtype=F32) / HEAD_DIM)
    ang = jnp.arange(seq, dtype=jnp.int32).astype(F32)[:, None] * inv_freq[None, :]
    ang = jnp.concatenate([ang, ang], axis=-1)
    cos, sin = jnp.cos(ang), jnp.sin(ang)
    lo = (jnp.arange(HEAD_DIM) < half)[None, :]
    sin_lo = jnp.where(lo, -sin, 0.0)
    sin_hi = jnp.where(lo, 0.0, sin)
    rep = LANES // HEAD_DIM
    k_tabs = [jnp.tile(t, (1, rep)) for t in (cos, sin_lo, sin_hi)]
    q_scale = HEAD_DIM ** -0.5 * math.log2(math.e)
    return jnp.concatenate([t * q_scale for t in k_tabs] + k_tabs, axis=1)


def kernel(x, g_mix_pre, w_in, lambda_q1, lambda_k1, lambda_q2, lambda_k2, g_sub,
           w_attn_o, w_dw, b_dw, ln_g, ln_b, w_conv_o, b_gate, w_out, g_mix_post,
           g_ffn_pre, w_up, w_fconv, b_fconv, w_down, g_ffn_post):
    bsz, seq, d_model = x.shape
    depth = w_in.shape[0]
    assert depth == 1, "LAM_INIT is specialised to a single layer"
    assert seq % TM_PROJ == 0 and seq % TM_FFN == 0 and seq % TQ == 0
    width = N_HEADS * V_HEAD_DIM
    ffn = w_fconv.shape[-1]
    rope_tab = _rope_tables(seq)
    x2d = x.reshape(bsz * seq, d_model)
    row2 = lambda a: a.reshape(1, -1)
    rows8 = lambda a: jnp.broadcast_to(a[..., None, :], a.shape[:-1] + (SUBLANES, a.shape[-1]))
    l = 0

    w_proj = w_in[l][:, :5 * width].astype(BF16)
    w_gate = w_in[l][:, 5 * width:].astype(BF16)

    qt, k, vt, c = _inproj(x2d, row2(g_mix_pre[l]), w_proj, rope_tab, seq)
    attn = _attention(qt, k, vt, seq, row2(lambda_q1[l]), row2(lambda_k1[l]), row2(lambda_q2[l]),
                      row2(lambda_k2[l]), g_sub[l].reshape(-1, 1))
    w_fc_pad = jnp.pad(w_fconv[l], ((0, SUBLANES - FFN_CONV_K), (0, 0)))
    consts = [rows8(w_dw[l]), rows8(b_dw[l]), row2(g_mix_pre[l]), w_gate, row2(b_gate[l]), row2(ln_g[l]), row2(ln_b[l]),
              w_attn_o[l].astype(BF16), w_conv_o[l].astype(BF16), w_out[l].astype(BF16),
              row2(g_mix_post[l]), row2(g_ffn_pre[l]), w_up[l][:, :ffn].astype(BF16),
              w_up[l][:, ffn:].astype(BF16), w_fc_pad, row2(b_fconv[l]), w_down[l].astype(BF16),
              row2(g_ffn_post[l])]
    x2d = _merge_ffn(x2d, attn, c, seq, consts)
    return x2d.reshape(bsz, seq, d_model)
```

```python
import functools
import math

import jax
import jax.numpy as jnp
from jax import lax
from jax.experimental import pallas as pl
from jax.experimental.pallas import tpu as pltpu

N_HEADS = 4
HEAD_DIM = 64
V_HEAD_DIM = 2 * HEAD_DIM
CHUNK = 64
CONV_K = 31
FFN_CONV_K = 3
ROPE_THETA = 10000.0
RMS_EPS = 1e-6
SUB_EPS = 1e-5
LN_EPS = 1e-5
LAM_INIT = 0.8 - 0.6 * math.exp(-0.3 * 0)
LANES = 128
SUBLANES = 8
VMEM_LIMIT = 56 * 1024 * 1024

TM_PROJ = 512
TM_FFN = 256
TQ = 256
DENOM_ROWS = 16
HALO = 32
CONV_ROWS = 64
NEG_BIG = -1e30

BF16 = jnp.bfloat16
F32 = jnp.float32


def _rms_norm(x, g, eps):
    ms = jnp.mean(x * x, axis=-1, keepdims=True)
    return (x * lax.rsqrt(ms + eps)) * g


def _const_spec(shape):
    nd = len(shape)
    return pl.BlockSpec(shape, lambda *_: (0,) * nd, pipeline_mode=pl.Buffered(1))


def _inproj_kernel(x_ref, g_ref, w_ref, rope_ref, qt_ref, k_ref, vt_ref, c_ref):
    tm, width = c_ref.shape
    h = _rms_norm(x_ref[...], g_ref[...], RMS_EPS).astype(BF16)
    z = jnp.dot(h, w_ref[...], preferred_element_type=F32)
    tabs = [rope_ref[:, t * LANES:(t + 1) * LANES] for t in range(6)]

    def rope(zs, cos, sin_lo, sin_hi):
        up = pltpu.roll(zs, LANES - HEAD_DIM // 2, 1)
        down = pltpu.roll(zs, HEAD_DIM // 2, 1)
        return zs * cos + up * sin_lo + down * sin_hi

    def store_transposed(dst_ref, s, val):
        for t in range(tm // TQ):
            dst_ref[t, s * LANES:(s + 1) * LANES, :] = val[t * TQ:(t + 1) * TQ, :].T.astype(dst_ref.dtype)

    for s in range(width // LANES):
        lanes = slice(s * LANES, (s + 1) * LANES)
        store_transposed(qt_ref, s, rope(z[:, lanes], *tabs[:3]))
        k_ref[s] = rope(z[:, width + s * LANES: width + (s + 1) * LANES], *tabs[3:]).astype(k_ref.dtype)
        store_transposed(vt_ref, s, z[:, 2 * width + s * LANES: 2 * width + (s + 1) * LANES])

    a = z[:, 3 * width:4 * width]
    b = z[:, 4 * width:5 * width]
    c_ref[...] = (a * jax.nn.sigmoid(b)).astype(c_ref.dtype)


def _inproj(x2d, g, w, rope_tab, seq):
    tokens, d_model = x2d.shape
    width = N_HEADS * V_HEAD_DIM
    tm = TM_PROJ
    tiles_per_seq = seq // tm
    row = lambda i: (i, 0)
    pos = lambda i: (i % tiles_per_seq, 0)
    nat = jax.ShapeDtypeStruct((tokens, width), BF16)
    tr = jax.ShapeDtypeStruct((tokens // TQ, width, TQ), BF16)
    tr_spec = pl.BlockSpec((tm // TQ, width, TQ), lambda i: (i, 0, 0))
    heads = jax.ShapeDtypeStruct((N_HEADS, tokens, V_HEAD_DIM), BF16)
    heads_spec = pl.BlockSpec((N_HEADS, tm, V_HEAD_DIM), lambda i: (0, i, 0))
    return pl.pallas_call(
        _inproj_kernel,
        grid=(tokens // tm,),
        in_specs=[
            pl.BlockSpec((tm, d_model), row),
            _const_spec((1, d_model)),
            _const_spec(w.shape),
            pl.BlockSpec((tm, rope_tab.shape[1]), pos),
        ],
        out_specs=[tr_spec, heads_spec, tr_spec, pl.BlockSpec((tm, width), row)],
        out_shape=[tr, heads, tr, nat],
        compiler_params=pltpu.CompilerParams(
            dimension_semantics=("parallel",), vmem_limit_bytes=VMEM_LIMIT),
        name="inproj",
    )(x2d, g, w, rope_tab)


def _attn_kernel(lq1_ref, lk1_ref, lq2_ref, lk2_ref, gsub_ref, qt_ref, k_ref, vt_ref,
                 o_ref, acc_ref, s_ref):
    nq = qt_ref.shape[0]
    tq = TQ
    lam = (jnp.exp(jnp.sum(lq1_ref[...] * lk1_ref[...], axis=-1, keepdims=True))
           - jnp.exp(jnp.sum(lq2_ref[...] * lk2_ref[...], axis=-1, keepdims=True))
           + LAM_INIT)
    gsub = gsub_ref[...] * (1.0 - LAM_INIT)

    key_chunk = lax.broadcasted_iota(jnp.int32, (tq, 2 * tq), 0) // CHUNK
    qry_chunk = (lax.broadcasted_iota(jnp.int32, (tq, 2 * tq), 1) & (tq - 1)) // CHUNK
    diag_mask = key_chunk <= qry_chunk
    zeros_half = jnp.zeros((HEAD_DIM, tq), BF16)

    def feats(h):
        return slice(h * V_HEAD_DIM, (h + 1) * V_HEAD_DIM)

    def scores(h, qs, j):
        kb = k_ref[h, pl.ds(pl.multiple_of(j * tq, tq), tq), :]
        return jnp.dot(kb, qs, preferred_element_type=F32)

    ones_rows = jnp.ones((DENOM_ROWS, tq), BF16)

    def accumulate(h, s, j, m, mask):
        if mask is not None:
            s = jnp.where(mask, s, NEG_BIG)
        m_new = jnp.maximum(m, jnp.max(s, axis=0, keepdims=True))
        alpha = jnp.exp2(m - m_new)
        p = jnp.exp2(s - m_new).astype(BF16)
        vb = jnp.concatenate([vt_ref[j, feats(h), :], ones_rows], axis=0)
        acc_ref[h] = alpha * acc_ref[h] + jnp.dot(vb, p, preferred_element_type=F32)
        return m_new

    def stacked_queries(i):
        qs = []
        for h in range(N_HEADS):
            qt = qt_ref[i, feats(h), :]
            qs.append(jnp.concatenate(
                [jnp.concatenate([qt[:HEAD_DIM], zeros_half], axis=0),
                 jnp.concatenate([zeros_half, qt[HEAD_DIM:]], axis=0)], axis=1))
        return qs

    def first_scores(i):
        qs = stacked_queries(i)
        for h in range(N_HEADS):
            s_ref[h] = scores(h, qs[h], 0)

    def q_block(i, carry):
        qs = stacked_queries(i)
        acc_ref[...] = jnp.zeros_like(acc_ref)
        m0 = jnp.full((1, 2 * tq), NEG_BIG, F32)

        def k_block(j, m):
            s_next = [scores(h, qs[h], j + 1) for h in range(N_HEADS)]
            m_new = []
            for h in range(N_HEADS):
                m_new.append(accumulate(h, s_ref[h], j, m[h], None))
                s_ref[h] = s_next[h]
            return tuple(m_new)

        m = lax.fori_loop(0, i, k_block, (m0,) * N_HEADS)
        for h in range(N_HEADS):
            accumulate(h, s_ref[h], i, m[h], diag_mask)
        first_scores(jnp.minimum(i + 1, nq - 1))
        for h in range(N_HEADS):
            acc = acc_ref[h]
            o = acc[:V_HEAD_DIM] / acc[V_HEAD_DIM:V_HEAD_DIM + 1]
            o = o[:, :tq] - lam * o[:, tq:]
            o = o * lax.rsqrt(jnp.mean(o * o, axis=0, keepdims=True) + SUB_EPS)
            o_ref[pl.ds(pl.multiple_of(i * tq, tq), tq), feats(h)] = (o * gsub).T.astype(o_ref.dtype)
        return carry

    first_scores(0)
    lax.fori_loop(0, nq, q_block, 0)


def _attention(qt, k, vt, seq, lq1, lk1, lq2, lk2, g_sub):
    tokens = k.shape[1]
    width = N_HEADS * V_HEAD_DIM
    nq = seq // TQ
    tr_spec = pl.BlockSpec((nq, width, TQ), lambda b: (b, 0, 0))
    nat_spec = pl.BlockSpec((seq, width), lambda b: (b, 0))
    heads_spec = pl.BlockSpec((N_HEADS, seq, V_HEAD_DIM), lambda b: (0, b, 0))
    vec = lambda shape: pl.BlockSpec(shape, lambda b: (0, 0))
    return pl.pallas_call(
        _attn_kernel,
        grid=(tokens // seq,),
        in_specs=[vec((1, HEAD_DIM))] * 4 + [vec((V_HEAD_DIM, 1)), tr_spec, heads_spec, tr_spec],
        out_specs=nat_spec,
        out_shape=jax.ShapeDtypeStruct((tokens, width), BF16),
        scratch_shapes=[pltpu.VMEM((N_HEADS, V_HEAD_DIM + DENOM_ROWS, 2 * TQ), F32),
                        pltpu.VMEM((N_HEADS, TQ, 2 * TQ), F32)],
        compiler_params=pltpu.CompilerParams(
            dimension_semantics=("parallel",), vmem_limit_bytes=VMEM_LIMIT),
        name="diff_attn",
    )(lq1, lk1, lq2, lk2, g_sub, qt, k, vt)


def _stage_shifted_rows(cbuf_ref, shift_ref):
    n8, _, width = shift_ref.shape[1:]
    for r in range(1, SUBLANES):
        shift_ref[r - 1] = cbuf_ref[pl.ds(r, n8 * SUBLANES), :].reshape(n8, SUBLANES, width)


def _depthwise_conv_rows(cbuf_ref, shift_ref, w_ref, b_ref, blk):
    width = cbuf_ref.shape[1]
    off = HALO - (CONV_K - 1)
    g8 = CONV_ROWS // SUBLANES
    acc = jnp.broadcast_to(b_ref[...][None], (g8, SUBLANES, width))
    for j in range(CONV_K):
        r, q8 = (off + j) % SUBLANES, (off + j) // SUBLANES
        if r == 0:
            tap = cbuf_ref[pl.ds(blk * CONV_ROWS + q8 * SUBLANES, CONV_ROWS), :].reshape(
                g8, SUBLANES, width)
        else:
            tap = shift_ref[r - 1, pl.ds(blk * g8 + q8, g8)]
        acc = acc + tap * w_ref[j]
    return acc.reshape(CONV_ROWS, width)


def _merge_ffn_kernel(tiles_per_seq, x_ref, attn_ref, c_ref, halo_ref, w_dw_ref, b_dw_ref,
                      g_pre_ref, w_gate_ref, b_gate_ref, ln_g_ref, ln_b_ref, w_attn_o_ref,
                      w_conv_o_ref, w_out_ref, g_post_ref, g_ffn_pre_ref, w_a_ref, w_b_ref,
                      w_fc_ref, b_fc_ref, w_down_ref, g_ffn_post_ref, o_ref,
                      abuf_ref, cbuf_ref, shift_ref):
    tm, d_model = x_ref.shape
    ffn = abuf_ref.shape[1]
    first = pl.program_id(0) % tiles_per_seq == 0

    @pl.when(first)
    def _():
        abuf_ref[0:SUBLANES, :] = jnp.zeros((SUBLANES, ffn), F32)

    x = x_ref[...]
    h = _rms_norm(x, g_pre_ref[...], RMS_EPS).astype(BF16)
    gate = jnp.dot(h, w_gate_ref[...], preferred_element_type=F32) + b_gate_ref[...]
    y_a = jnp.dot(attn_ref[...], w_attn_o_ref[...], preferred_element_type=F32)

    halo = halo_ref[...].astype(F32)
    cbuf_ref[0:HALO, :] = jnp.where(first, jnp.zeros_like(halo), halo)
    cbuf_ref[HALO:, :] = c_ref[...].astype(F32)
    _stage_shifted_rows(cbuf_ref, shift_ref)
    conv = jnp.concatenate(
        [_depthwise_conv_rows(cbuf_ref, shift_ref, w_dw_ref, b_dw_ref, blk)
         for blk in range(tm // CONV_ROWS)], axis=0)
    mu = jnp.mean(conv, axis=-1, keepdims=True)
    cen = conv - mu
    var = jnp.mean(cen * cen, axis=-1, keepdims=True)
    y = cen * lax.rsqrt(var + LN_EPS) * ln_g_ref[...] + ln_b_ref[...]
    y = (y * jax.nn.sigmoid(y)).astype(BF16)
    y_c = jnp.dot(y, w_conv_o_ref[...], preferred_element_type=F32)

    gate = jax.nn.sigmoid(gate)
    m = (gate[:, :d_model] * y_a + gate[:, d_model:] * y_c).astype(BF16)
    o = jnp.dot(m, w_out_ref[...], preferred_element_type=F32)
    x = x + _rms_norm(o, g_post_ref[...], RMS_EPS)

    h = _rms_norm(x, g_ffn_pre_ref[...], RMS_EPS).astype(BF16)
    abuf_ref[SUBLANES:, :] = jnp.dot(h, w_a_ref[...], preferred_element_type=F32)
    b = jnp.dot(h, w_b_ref[...], preferred_element_type=F32)
    w_fc = w_fc_ref[...]
    off = SUBLANES - (FFN_CONV_K - 1)
    a = b_fc_ref[...] + abuf_ref[pl.ds(off, tm), :] * w_fc[0:1, :]
    for j in range(1, FFN_CONV_K):
        a = a + abuf_ref[pl.ds(off + j, tm), :] * w_fc[j:j + 1, :]
    abuf_ref[0:SUBLANES, :] = abuf_ref[tm:tm + SUBLANES, :]
    p = (jax.nn.gelu(a, approximate=True) * b).astype(BF16)
    f = jnp.dot(p, w_down_ref[...], preferred_element_type=F32)
    o_ref[...] = x + _rms_norm(f, g_ffn_post_ref[...], RMS_EPS)


def _merge_ffn(x2d, attn2d, c2d, seq, consts):
    tokens, d_model = x2d.shape
    width = attn2d.shape[1]
    ffn = consts[-2].shape[0]
    tm = TM_FFN
    tiles_per_seq = seq // tm
    row = lambda i: (i, 0)
    halo = lambda i: (jnp.maximum(i * (tm // HALO) - 1, 0), 0)
    n8 = (tm + HALO) // SUBLANES - 1
    return pl.pallas_call(
        functools.partial(_merge_ffn_kernel, tiles_per_seq),
        grid=(tokens // tm,),
        in_specs=[
            pl.BlockSpec((tm, d_model), row),
            pl.BlockSpec((tm, width), row),
            pl.BlockSpec((tm, width), row),
            pl.BlockSpec((HALO, width), halo),
        ] + [_const_spec(c.shape) for c in consts],
        out_specs=pl.BlockSpec((tm, d_model), row),
        out_shape=jax.ShapeDtypeStruct((tokens, d_model), F32),
        scratch_shapes=[pltpu.VMEM((tm + SUBLANES, ffn), F32),
                        pltpu.VMEM((tm + HALO, width), F32),
                        pltpu.VMEM((SUBLANES - 1, n8, SUBLANES, width), F32)],
        compiler_params=pltpu.CompilerParams(
            dimension_semantics=("arbitrary",), vmem_limit_bytes=VMEM_LIMIT),
        name="merge_ffn",
    )(x2d, attn2d, c2d, c2d, *consts)


def _rope_tables(seq):
    half = HEAD_DIM // 2
    inv_freq = ROPE_THETA ** (-jnp.arange(0, HEAD_DIM, 2, dtype=F32) / HEAD_DIM)
    ang = jnp.arange(seq, dtype=jnp.int32).astype(F32)[:, None] * inv_freq[None, :]
    ang = jnp.concatenate([ang, ang], axis=-1)
    cos, sin = jnp.cos(ang), jnp.sin(ang)
    lo = (jnp.arange(HEAD_DIM) < half)[None, :]
    sin_lo = jnp.where(lo, -sin, 0.0)
    sin_hi = jnp.where(lo, 0.0, sin)
    rep = LANES // HEAD_DIM
    k_tabs = [jnp.tile(t, (1, rep)) for t in (cos, sin_lo, sin_hi)]
    q_scale = HEAD_DIM ** -0.5 * math.log2(math.e)
    return jnp.concatenate([t * q_scale for t in k_tabs] + k_tabs, axis=1)


def kernel(x, g_mix_pre, w_in, lambda_q1, lambda_k1, lambda_q2, lambda_k2, g_sub,
           w_attn_o, w_dw, b_dw, ln_g, ln_b, w_conv_o, b_gate, w_out, g_mix_post,
           g_ffn_pre, w_up, w_fconv, b_fconv, w_down, g_ffn_post):
    bsz, seq, d_model = x.shape
    depth = w_in.shape[0]
    assert depth == 1, "LAM_INIT is specialised to a single layer"
    assert seq % TM_PROJ == 0 and seq % TM_FFN == 0 and seq % TQ == 0
    width = N_HEADS * V_HEAD_DIM
    ffn = w_fconv.shape[-1]
    rope_tab = _rope_tables(seq)
    x2d = x.reshape(bsz * seq, d_model)
    row2 = lambda a: a.reshape(1, -1)
    rows8 = lambda a: jnp.broadcast_to(a[..., None, :], a.shape[:-1] + (SUBLANES, a.shape[-1]))
    l = 0

    w_proj = w_in[l][:, :5 * width].astype(BF16)
    w_gate = w_in[l][:, 5 * width:].astype(BF16)

    qt, k, vt, c = _inproj(x2d, row2(g_mix_pre[l]), w_proj, rope_tab, seq)
    attn = _attention(qt, k, vt, seq, row2(lambda_q1[l]), row2(lambda_k1[l]), row2(lambda_q2[l]),
                      row2(lambda_k2[l]), g_sub[l].reshape(-1, 1))
    w_fc_pad = jnp.pad(w_fconv[l], ((0, SUBLANES - FFN_CONV_K), (0, 0)))
    consts = [rows8(w_dw[l]), rows8(b_dw[l]), row2(g_mix_pre[l]), w_gate, row2(b_gate[l]), row2(ln_g[l]), row2(ln_b[l]),
              w_attn_o[l].astype(BF16), w_conv_o[l].astype(BF16), w_out[l].astype(BF16),
              row2(g_mix_post[l]), row2(g_ffn_pre[l]), w_up[l][:, :ffn].astype(BF16),
              w_up[l][:, ffn:].astype(BF16), w_fc_pad, row2(b_fconv[l]), w_down[l].astype(BF16),
              row2(g_ffn_post[l])]
    x2d = _merge_ffn(x2d, attn, c, seq, consts)
    return x2d.reshape(bsz, seq, d_model)
```
